```python
import math
import jax
import jax.numpy as jnp
from jax import lax
import numpy as np

D_MODEL = 1024
BATCH = 16
SEQ = 2048
DEPTH = 1

CHUNK = 64
EPS = 1e-6

ATTN_HEAD_DIM = 64
ATTN_WIDTH = D_MODEL // 2
ATTN_HEADS = ATTN_WIDTH // ATTN_HEAD_DIM
LEFT_CHUNKS = 8
BAND = (LEFT_CHUNKS + 1) * CHUNK
REL_CLIP = 128
N_REL = 2 * REL_CLIP + 1

DN_HEAD_DIM = 128
DN_WIDTH = D_MODEL - ATTN_WIDTH
DN_HEADS = DN_WIDTH // DN_HEAD_DIM
CONV_WIDTH = 4

MIX_WIDTH = ATTN_WIDTH + DN_WIDTH
IN_PROJ_WIDTH = 3 * ATTN_WIDTH + 4 * DN_WIDTH + 2 * DN_HEADS

PEER_HEADS = 8
PEER_KEY_DIM = 256
PEER_HALF = PEER_KEY_DIM // 2
N_KEYS = 128
N_EXPERTS = N_KEYS * N_KEYS
PEER_TOPK = 16
PEER_BLOCK = 128

kernel_name = "hymba_chunkattn_gdn_peer_adaln"


def rms_norm(x, gain):
    xf = x.astype(jnp.float32)
    y = xf * lax.rsqrt(jnp.mean(xf * xf, axis=-1, keepdims=True) + EPS)
    return (y * gain.astype(jnp.float32)).astype(x.dtype)


def l2_norm(x):
    xf = x.astype(jnp.float32)
    return xf * lax.rsqrt(jnp.sum(xf * xf, axis=-1, keepdims=True) + EPS)


def modulate(h, shift, scale):
    return h * (1.0 + scale) + shift


def chunked_band_attention(q, k, v, q_gain, k_gain, rel_bias):
    B, S = q.shape[:2]
    nc = S // CHUNK
    shp = (B, nc, CHUNK, ATTN_HEADS, ATTN_HEAD_DIM)
    pad = ((0, 0), (LEFT_CHUNKS, 0), (0, 0), (0, 0), (0, 0))
    q = rms_norm(q, q_gain).reshape(shp)
    k = jnp.pad(rms_norm(k, k_gain).reshape(shp), pad)
    v = jnp.pad(v.reshape(shp), pad)
    band_idx = np.arange(nc)[:, None] + np.arange(LEFT_CHUNKS + 1)[None, :]
    kb = k[:, band_idx].reshape(B, nc, BAND, ATTN_HEADS, ATTN_HEAD_DIM)
    vb = v[:, band_idx].reshape(B, nc, BAND, ATTN_HEADS, ATTN_HEAD_DIM)
    scores = jnp.einsum("bnqhd,bnkhd->bhnqk", q, kb,
                        preferred_element_type=jnp.float32) * (ATTN_HEAD_DIM ** -0.5)
    rel = np.arange(CHUNK)[:, None] + LEFT_CHUNKS * CHUNK - np.arange(BAND)[None, :]
    rel_idx = np.clip(rel, -REL_CLIP, REL_CLIP) + REL_CLIP
    bias = rel_bias.astype(jnp.float32)[:, rel_idx]
    valid = (np.arange(nc)[:, None] + np.arange(BAND)[None, :] // CHUNK) >= LEFT_CHUNKS
    scores = jnp.where(valid[None, None, :, None, :], scores + bias[None, :, None], -jnp.inf)
    p = jax.nn.softmax(scores, axis=-1).astype(vb.dtype)
    o = jnp.einsum("bhnqk,bnkhd->bnqhd", p, vb)
    return o.reshape(B, S, ATTN_WIDTH)


def causal_depthwise_conv(x, w):
    S = x.shape[1]
    xp = jnp.pad(x, ((0, 0), (CONV_WIDTH - 1, 0), (0, 0)))
    y = xp[:, 0:S] * w[0]
    for i in range(1, CONV_WIDTH):
        y = y + xp[:, i:i + S] * w[i]
    return jax.nn.silu(y)


def gated_delta_rule(q, k, v, g, beta):
    B, S, H, Dk = q.shape
    Dv = v.shape[-1]
    nc = S // CHUNK

    def to_chunks(t):
        return t.astype(jnp.float32).reshape(B, nc, CHUNK, H, -1).transpose(1, 0, 3, 2, 4)

    q = to_chunks(q) * (Dk ** -0.5)
    k = to_chunks(k)
    v = to_chunks(v)
    g = g.astype(jnp.float32).reshape(B, nc, CHUNK, H).transpose(1, 0, 3, 2)
    beta = beta.astype(jnp.float32).reshape(B, nc, CHUNK, H).transpose(1, 0, 3, 2)
    gc = jnp.cumsum(g, axis=-1)
    tri = np.tril(np.ones((CHUNK, CHUNK), dtype=bool))
    strict = np.tril(np.ones((CHUNK, CHUNK), dtype=bool), -1)
    decay = jnp.exp(jnp.where(tri, gc[..., :, None] - gc[..., None, :], -jnp.inf))
    k_beta = k * beta[..., None]
    v_beta = v * beta[..., None]
    lower = jnp.where(strict, jnp.einsum("nbhid,nbhjd->nbhij", k_beta, k) * decay, 0.0)
    rhs = jnp.concatenate([v_beta, k_beta * jnp.exp(gc)[..., None]], axis=-1)
    sol = lax.linalg.triangular_solve(jnp.eye(CHUNK, dtype=jnp.float32) + lower, rhs,
                                      left_side=True, lower=True, unit_diagonal=True)
    u = sol[..., :Dv]
    w = sol[..., Dv:]
    intra = jnp.einsum("nbhid,nbhjd->nbhij", q, k) * decay

    def step(state, xs):
        q_c, k_c, u_c, w_c, a_c, g_c = xs
        v_new = u_c - jnp.einsum("bhcd,bhde->bhce", w_c, state)
        o = (jnp.einsum("bhcd,bhde->bhce", q_c * jnp.exp(g_c)[..., None], state)
             + jnp.einsum("bhij,bhje->bhie", a_c, v_new))
        g_last = g_c[..., -1]
        k_dec = k_c * jnp.exp(g_last[..., None] - g_c)[..., None]
        state = state * jnp.exp(g_last)[..., None, None] + jnp.einsum("bhcd,bhce->bhde", k_dec, v_new)
        return state, o

    state0 = jnp.zeros((B, H, Dk, Dv), jnp.float32)
    _, o = lax.scan(step, state0, (q, k, u, w, intra, gc))
    return o.transpose(1, 0, 3, 2, 4).reshape(B, S, H, Dv)


def peer(h, w_query, query_gain, sub_keys_1, sub_keys_2, expert_down, expert_up):
    B, S, D = h.shape
    n_blocks = (B * S) // PEER_BLOCK

    def block(hb):
        qry = (hb @ w_query).reshape(PEER_BLOCK, PEER_HEADS, PEER_KEY_DIM)
        qry = rms_norm(qry, query_gain)
        s1 = jnp.einsum("thd,nd->thn", qry[..., :PEER_HALF], sub_keys_1, preferred_element_type=jnp.float32)
        s2 = jnp.einsum("thd,nd->thn", qry[..., PEER_HALF:], sub_keys_2, preferred_element_type=jnp.float32)
        v1, i1 = lax.top_k(s1, PEER_TOPK)
        v2, i2 = lax.top_k(s2, PEER_TOPK)
        cand_s = (v1[..., :, None] + v2[..., None, :]).reshape(PEER_BLOCK, PEER_HEADS, PEER_TOPK * PEER_TOPK)
        cand_i = (i1[..., :, None] * N_KEYS + i2[..., None, :]).reshape(PEER_BLOCK, PEER_HEADS, PEER_TOPK * PEER_TOPK)
        top_s, pos = lax.top_k(cand_s, PEER_TOPK)
        experts = jnp.take_along_axis(cand_i, pos, axis=-1)
        gates = jax.nn.softmax(top_s, axis=-1)
        u = expert_down[experts]
        vv = expert_up[experts]
        act = jax.nn.gelu(jnp.einsum("thkd,td->thk", u, hb, preferred_element_type=jnp.float32),
                          approximate=False)
        return jnp.einsum("thk,thkd->td", (gates * act).astype(vv.dtype), vv)

    out = lax.map(block, h.reshape(n_blocks, PEER_BLOCK, D))
    return out.reshape(B, S, D)


def hybrid_layer(x, c, w_ada, b_ada, norm1_gain, w_in, attn_q_gain, attn_k_gain, attn_rel_bias,
                 attn_out_gain, dn_conv_w, dn_a_log, dn_dt_bias, dn_out_gain, w_out, norm2_gain,
                 peer_w_query, peer_query_gain, peer_sub_keys_1, peer_sub_keys_2,
                 peer_expert_down, peer_expert_up):
    B, S, _ = x.shape
    mod = jax.nn.silu(c) @ w_ada + b_ada
    shift1, scale1, gate1, shift2, scale2, gate2 = [m[:, None, :] for m in jnp.split(mod, 6, axis=-1)]

    h = modulate(rms_norm(x, norm1_gain), shift1, scale1)
    proj = h @ w_in
    cuts = [ATTN_WIDTH, 2 * ATTN_WIDTH, 3 * ATTN_WIDTH, 3 * ATTN_WIDTH + 3 * DN_WIDTH,
            3 * ATTN_WIDTH + 4 * DN_WIDTH, 3 * ATTN_WIDTH + 4 * DN_WIDTH + DN_HEADS]
    qa, ka, va, qkv_b, z_b, b_raw, a_raw = jnp.split(proj, cuts, axis=-1)

    hd_a = (B, S, ATTN_HEADS, ATTN_HEAD_DIM)
    o_a = chunked_band_attention(qa.reshape(hd_a), ka.reshape(hd_a), va.reshape(hd_a),
                                 attn_q_gain, attn_k_gain, attn_rel_bias)
    o_a = rms_norm(o_a, attn_out_gain)

    qkv_b = causal_depthwise_conv(qkv_b, dn_conv_w)
    q_b, k_b, v_b = jnp.split(qkv_b, 3, axis=-1)
    hd_b = (B, S, DN_HEADS, DN_HEAD_DIM)
    q_b = l2_norm(q_b.reshape(hd_b))
    k_b = l2_norm(k_b.reshape(hd_b))
    beta = jax.nn.sigmoid(b_raw.astype(jnp.float32))
    g = -jnp.exp(dn_a_log.astype(jnp.float32)) * jax.nn.softplus(
        a_raw.astype(jnp.float32) + dn_dt_bias.astype(jnp.float32))
    o_b = gated_delta_rule(q_b, k_b, v_b.reshape(hd_b), g, beta)
    o_b = rms_norm(o_b, dn_out_gain) * jax.nn.silu(z_b.reshape(hd_b).astype(jnp.float32))
    o_b = o_b.reshape(B, S, DN_WIDTH).astype(x.dtype)

    mix = jnp.concatenate([o_a, o_b], axis=-1) @ w_out
    x = x + gate1 * mix

    h2 = modulate(rms_norm(x, norm2_gain), shift2, scale2)
    y = peer(h2, peer_w_query, peer_query_gain, peer_sub_keys_1, peer_sub_keys_2,
             peer_expert_down, peer_expert_up)
    return x + gate2 * y


def setup_inputs(seed: int = 0) -> dict:
    key = jax.random.key(seed)
    ks = iter(jax.random.split(key, 32))
    L = DEPTH

    def nrm(shape, scale):
        return scale * jax.random.normal(next(ks), shape, jnp.float32)

    def gain(shape):
        return 1.0 + nrm(shape, 0.02)

    x = nrm((BATCH, SEQ, D_MODEL), 1.0)
    c = nrm((BATCH, D_MODEL), 1.0)
    w_ada = nrm((L, D_MODEL, 6 * D_MODEL), 0.5 * D_MODEL ** -0.5)
    b_ada = nrm((L, 6 * D_MODEL), 0.02)
    norm1_gain = gain((L, D_MODEL))
    w_in = nrm((L, D_MODEL, IN_PROJ_WIDTH), D_MODEL ** -0.5)
    attn_q_gain = gain((L, ATTN_HEAD_DIM))
    attn_k_gain = gain((L, ATTN_HEAD_DIM))
    attn_rel_bias = nrm((L, ATTN_HEADS, N_REL), 0.5)
    attn_out_gain = gain((L, ATTN_WIDTH))
    dn_conv_w = nrm((L, CONV_WIDTH, 3 * DN_WIDTH), CONV_WIDTH ** -0.5)
    dn_a_log = jnp.log(jax.random.uniform(next(ks), (L, DN_HEADS), jnp.float32, 1.0, 16.0))
    dt = jnp.exp(jax.random.uniform(next(ks), (L, DN_HEADS), jnp.float32,
                                    math.log(1e-3), math.log(1e-1)))
    dn_dt_bias = dt + jnp.log(-jnp.expm1(-dt))
    dn_out_gain = gain((L, DN_HEAD_DIM))
    w_out = nrm((L, MIX_WIDTH, D_MODEL), MIX_WIDTH ** -0.5)
    norm2_gain = gain((L, D_MODEL))
    peer_w_query = nrm((L, D_MODEL, PEER_HEADS * PEER_KEY_DIM), D_MODEL ** -0.5)
    peer_query_gain = gain((L, PEER_KEY_DIM))
    peer_sub_keys_1 = nrm((L, N_KEYS, PEER_HALF), PEER_HALF ** -0.5)
    peer_sub_keys_2 = nrm((L, N_KEYS, PEER_HALF), PEER_HALF ** -0.5)
    peer_expert_down = nrm((L, N_EXPERTS, D_MODEL), D_MODEL ** -0.5)
    peer_expert_up = nrm((L, N_EXPERTS, D_MODEL), 0.5)
    return {"x": x, "c": c, "w_ada": w_ada, "b_ada": b_ada, "norm1_gain": norm1_gain,
            "w_in": w_in, "attn_q_gain": attn_q_gain, "attn_k_gain": attn_k_gain,
            "attn_rel_bias": attn_rel_bias, "attn_out_gain": attn_out_gain,
            "dn_conv_w": dn_conv_w, "dn_a_log": dn_a_log, "dn_dt_bias": dn_dt_bias,
            "dn_out_gain": dn_out_gain, "w_out": w_out, "norm2_gain": norm2_gain,
            "peer_w_query": peer_w_query, "peer_query_gain": peer_query_gain,
            "peer_sub_keys_1": peer_sub_keys_1, "peer_sub_keys_2": peer_sub_keys_2,
            "peer_expert_down": peer_expert_down, "peer_expert_up": peer_expert_up}


def reference(x, c, w_ada, b_ada, norm1_gain, w_in, attn_q_gain, attn_k_gain, attn_rel_bias,
              attn_out_gain, dn_conv_w, dn_a_log, dn_dt_bias, dn_out_gain, w_out, norm2_gain,
              peer_w_query, peer_query_gain, peer_sub_keys_1, peer_sub_keys_2,
              peer_expert_down, peer_expert_up):
    for layer in range(DEPTH):
        x = hybrid_layer(x, c, w_ada[layer], b_ada[layer], norm1_gain[layer], w_in[layer],
                         attn_q_gain[layer], attn_k_gain[layer], attn_rel_bias[layer],
                         attn_out_gain[layer], dn_conv_w[layer], dn_a_log[layer], dn_dt_bias[layer],
                         dn_out_gain[layer], w_out[layer], norm2_gain[layer], peer_w_query[layer],
                         peer_query_gain[layer], peer_sub_keys_1[layer], peer_sub_keys_2[layer],
                         peer_expert_down[layer], peer_expert_up[layer])
    return x
```

```python
import functools

import jax
import jax.numpy as jnp
from jax import lax
from jax.experimental import pallas as pl
from jax.experimental.pallas import tpu as pltpu

EPS = 1e-6
CHUNK = 64
LEFT_CHUNKS = 8
BAND = (LEFT_CHUNKS + 1) * CHUNK
REL_CLIP = 128
ATTN_HEAD_DIM = 64
DN_HEAD_DIM = 128
CONV_WIDTH = 4
PEER_HEADS = 8
PEER_HALF = 128
N_KEYS = 128
PEER_TOPK = 16

LANES = 128
SUBLANES = 8
VMEM_LIMIT = 56 * 1024 * 1024

_ROW_TILE = 512
_PEER_TOKEN_TILE = 512
_PEER_EXPERT_TILE = 2048

F32 = jnp.float32
BF16 = jnp.bfloat16
HI = lax.Precision.HIGHEST
NEG_INF = float("-inf")


def _cparams(*sem):
    return pltpu.CompilerParams(dimension_semantics=sem, vmem_limit_bytes=VMEM_LIMIT)


def _dot(a, b):
    return jnp.dot(a, b, preferred_element_type=F32)


def _dot_nt(a, b):
    return lax.dot_general(a, b, (((1,), (1,)), ((), ())), preferred_element_type=F32)


def _dot_tn(a, b):
    return lax.dot_general(a, b, (((0,), (0,)), ((), ())), preferred_element_type=F32)


def _split_bf16(a):
    hi = a.astype(BF16)
    lo = (a - hi.astype(F32)).astype(BF16)
    return hi, lo


def _dot3(a, b):
    ah, al = _split_bf16(a)
    bh, bl = _split_bf16(b)
    return _dot(ah, bh) + (_dot(al, bh) + _dot(ah, bl))


def _dot3_nt(a, b):
    ah, al = _split_bf16(a)
    bh, bl = _split_bf16(b)
    return _dot_nt(ah, bh) + (_dot_nt(al, bh) + _dot_nt(ah, bl))


def _ada_kernel(c_ref, w_ref, b_ref, o_ref):
    c = c_ref[...]
    a = c * jax.nn.sigmoid(c)
    o_ref[...] = jnp.dot(a, w_ref[...], precision=HI, preferred_element_type=F32) + b_ref[...]


def _ada(c, w_ada, b_ada):
    B, D = c.shape
    N = w_ada.shape[1]
    tn = 1024
    return pl.pallas_call(
        _ada_kernel,
        grid=(N // tn,),
        in_specs=[pl.BlockSpec((B, D), lambda j: (0, 0)),
                  pl.BlockSpec((D, tn), lambda j: (0, j)),
                  pl.BlockSpec((1, tn), lambda j: (0, j))],
        out_specs=pl.BlockSpec((B, tn), lambda j: (0, j)),
        out_shape=jax.ShapeDtypeStruct((B, N), F32),
        compiler_params=_cparams("arbitrary"),
    )(c, w_ada, b_ada.reshape(1, N))


def _inproj_kernel(x_ref, mod_ref, g_ref, wa_ref, wb_ref, wz_ref, wsh_ref, wsl_ref,
                   qkva_ref, qkvb_ref, z_ref, ba_ref):
    x = x_ref[...]
    shift = mod_ref[0, 0:1, :]
    scale = mod_ref[0, 1:2, :]
    y = x * lax.rsqrt(jnp.mean(x * x, axis=-1, keepdims=True) + EPS) * g_ref[...]
    h = y * (1.0 + scale) + shift
    hh, hl = _split_bf16(h)
    qkva_ref[...] = _dot(hh, wa_ref[...]).astype(qkva_ref.dtype)
    qkvb_ref[...] = _dot(hh, wb_ref[...]).astype(qkvb_ref.dtype)
    z_ref[...] = _dot(hh, wz_ref[...]).astype(z_ref.dtype)
    wsh = wsh_ref[...]
    ba_ref[...] = _dot(hh, wsh) + (_dot(hl, wsh) + _dot(hh, wsl_ref[...]))


def _inproj(x2, mod3, norm1_gain, w_in, S, aw, bw, n_small):
    T, D = x2.shape
    tm = min(_ROW_TILE, S)
    steps_per_b = S // tm
    wa = w_in[:, :3 * aw].astype(BF16)
    wb = w_in[:, 3 * aw:3 * aw + 3 * bw].astype(BF16)
    wz = w_in[:, 3 * aw + 3 * bw:3 * aw + 4 * bw].astype(BF16)
    ws = jnp.pad(w_in[:, 3 * aw + 4 * bw:], ((0, 0), (0, LANES - n_small)))
    wsh = ws.astype(BF16)
    wsl = (ws - wsh.astype(F32)).astype(BF16)
    const = lambda i: (0, 0)
    row = lambda i: (i, 0)
    return pl.pallas_call(
        _inproj_kernel,
        grid=(T // tm,),
        in_specs=[pl.BlockSpec((tm, D), row),
                  pl.BlockSpec((1, 6, D), lambda i: (i // steps_per_b, 0, 0)),
                  pl.BlockSpec((1, D), const),
                  pl.BlockSpec(wa.shape, const),
                  pl.BlockSpec(wb.shape, const),
                  pl.BlockSpec(wz.shape, const),
                  pl.BlockSpec(wsh.shape, const),
                  pl.BlockSpec(wsl.shape, const)],
        out_specs=[pl.BlockSpec((tm, 3 * aw), row),
                   pl.BlockSpec((tm, 3 * bw), row),
                   pl.BlockSpec((tm, bw), row),
                   pl.BlockSpec((tm, LANES), row)],
        out_shape=[jax.ShapeDtypeStruct((T, 3 * aw), BF16),
                   jax.ShapeDtypeStruct((T, 3 * bw), BF16),
                   jax.ShapeDtypeStruct((T, bw), BF16),
                   jax.ShapeDtypeStruct((T, LANES), F32)],
        compiler_params=_cparams("arbitrary"),
    )(x2, mod3, norm1_gain.reshape(1, D), wa, wb, wz, wsh, wsl)


_ATTN_CHUNKS_PER_ITER = 8


def _attn_kernel(q_ref, k_ref, v_ref, qg_ref, kg_ref, bias_ref, o_ref, qn_ref, kp_ref, vp_ref,
                 *, n_chunks):
    S = n_chunks * CHUNK
    pad = LEFT_CHUNKS * CHUNK
    lane = lax.broadcasted_iota(jnp.int32, (1, LANES), 1)
    first = lane < ATTN_HEAD_DIM

    same_head = (lax.broadcasted_iota(jnp.int32, (LANES, LANES), 0) < ATTN_HEAD_DIM) == (
        lax.broadcasted_iota(jnp.int32, (LANES, LANES), 1) < ATTN_HEAD_DIM)
    head_ones = same_head.astype(F32).astype(BF16)

    def head_norm(t, gain):
        sh, sl = _split_bf16(t * t)
        ms = (_dot(sh, head_ones) + _dot(sl, head_ones)) * (1.0 / ATTN_HEAD_DIM)
        return t * lax.rsqrt(ms + EPS) * gain

    qn = head_norm(q_ref[0].astype(F32), qg_ref[...]) * (ATTN_HEAD_DIM ** -0.5)
    qn_ref[...] = qn.astype(BF16)
    kp_ref[0:pad, :] = jnp.zeros((pad, LANES), BF16)
    vp_ref[0:pad, :] = jnp.zeros((pad, LANES), BF16)
    kp_ref[pad:pad + S, :] = head_norm(k_ref[0].astype(F32), kg_ref[...]).astype(BF16)
    vp_ref[pad:pad + S, :] = v_ref[0].astype(BF16)

    band_chunk = jnp.right_shift(lax.broadcasted_iota(jnp.int32, (2 * CHUNK, BAND), 1),
                                 CHUNK.bit_length() - 1)

    per_iter = min(_ATTN_CHUNKS_PER_ITER, n_chunks)

    def chunk_body(nb, carry):
        chunks = [nb * per_iter + k for k in range(per_iter)]
        r0 = [pl.multiple_of(n * CHUNK, CHUNK) for n in chunks]
        qc = [qn_ref[pl.ds(r, CHUNK), :] for r in r0]
        kb = [kp_ref[pl.ds(r, BAND), :] for r in r0]
        vb = [vp_ref[pl.ds(r, BAND), :] for r in r0]
        valid = [(band_chunk + n) >= LEFT_CHUNKS for n in chunks]
        qm = [jnp.concatenate([jnp.where(first, q, jnp.zeros_like(q)),
                               jnp.where(first, jnp.zeros_like(q), q)], axis=0) for q in qc]
        s = [_dot_nt(q, k) + bias_ref[...] for q, k in zip(qm, kb)]
        s = [jnp.where(v, x, NEG_INF) for x, v in zip(s, valid)]
        m = [jnp.max(x, axis=-1, keepdims=True) for x in s]
        p = [jnp.exp(x - mx) for x, mx in zip(s, m)]
        l = [jnp.sum(x, axis=-1, keepdims=True) for x in p]
        o = [_dot(x.astype(BF16), v) / d for x, v, d in zip(p, vb, l)]
        for k in range(per_iter):
            o_ref[0, pl.ds(r0[k], CHUNK), :] = jnp.where(first, o[k][:CHUNK], o[k][CHUNK:])
        return carry

    lax.fori_loop(0, n_chunks // per_iter, chunk_body, 0)


def _attention(qkva, q_gain, k_gain, rel_bias, B, S, aw):
    n_pairs = aw // LANES
    n_chunks = S // CHUNK
    diag = jnp.arange(-(CHUNK - 1), BAND)
    diag_idx = jnp.clip(LEFT_CHUNKS * CHUNK - diag, -REL_CLIP, REL_CLIP) + REL_CLIP
    ext = rel_bias.astype(F32)[:, diag_idx]
    bias = jnp.stack([lax.slice_in_dim(ext, CHUNK - 1 - q, CHUNK - 1 - q + BAND, axis=1)
                      for q in range(CHUNK)], axis=1)
    bias = bias.reshape(-1, BAND)
    qg2 = jnp.tile(q_gain.astype(F32), 2).reshape(1, LANES)
    kg2 = jnp.tile(k_gain.astype(F32), 2).reshape(1, LANES)
    qkv3 = qkva.reshape(B, S, 3 * aw)
    blk = (1, S, LANES)
    return pl.pallas_call(
        functools.partial(_attn_kernel, n_chunks=n_chunks),
        grid=(B, n_pairs),
        in_specs=[pl.BlockSpec(blk, lambda b, p: (b, 0, p)),
                  pl.BlockSpec(blk, lambda b, p: (b, 0, n_pairs + p)),
                  pl.BlockSpec(blk, lambda b, p: (b, 0, 2 * n_pairs + p)),
                  pl.BlockSpec((1, LANES), lambda b, p: (0, 0)),
                  pl.BlockSpec((1, LANES), lambda b, p: (0, 0)),
                  pl.BlockSpec((2 * CHUNK, BAND), lambda b, p: (p, 0))],
        out_specs=pl.BlockSpec(blk, lambda b, p: (b, 0, p)),
        out_shape=jax.ShapeDtypeStruct((B, S, aw), F32),
        scratch_shapes=[pltpu.VMEM((S, LANES), BF16),
                        pltpu.VMEM((S + LEFT_CHUNKS * CHUNK, LANES), BF16),
                        pltpu.VMEM((S + LEFT_CHUNKS * CHUNK, LANES), BF16)],
        compiler_params=_cparams("arbitrary", "arbitrary"),
    )(qkv3, qkv3, qkv3, qg2, kg2, bias)


_DN_CHUNKS_PER_ITER = 4


def _softplus(x):
    return jnp.maximum(x, 0.0) + jnp.log1p(jnp.exp(-jnp.abs(x)))


def _dn_kernel(q_ref, k_ref, v_ref, z_ref, ba_ref, cq_ref, ck_ref, cv_ref, alog_ref, dtb_ref,
               og_ref, o_ref, u_s, w_s, qg_s, kd_s, a_s, e_s, st_s, *, n_chunks, n_heads):
    C = CHUNK
    H = n_heads
    ri = lax.broadcasted_iota(jnp.int32, (C, C), 0)
    ci = lax.broadcasted_iota(jnp.int32, (C, C), 1)
    tril = ri >= ci
    strict = ri > ci
    tril_f = tril.astype(F32)
    triu_f = (ri <= ci).astype(F32)
    eye_f = (ri == ci).astype(F32)
    a_neg = -jnp.exp(alog_ref[...])
    dtb = dtb_ref[...]

    def conv_silu(x_ref, w_ref, h, r0, rp, c):
        sl = slice(h * LANES, (h + 1) * LANES)
        cur = x_ref[0, pl.ds(r0, C), sl].astype(F32)
        prev = x_ref[0, pl.ds(rp, 2 * SUBLANES), sl].astype(F32)[SUBLANES:]
        prev = jnp.where(c > 0, prev, 0.0)
        ext = jnp.concatenate([prev, cur], axis=0)
        y = cur * w_ref[CONV_WIDTH - 1:CONV_WIDTH, sl]
        for sft in range(1, CONV_WIDTH):
            xs = pltpu.roll(ext, sft, 0)[SUBLANES:, :]
            y = y + xs * w_ref[CONV_WIDTH - 1 - sft:CONV_WIDTH - sft, sl]
        return y * jax.nn.sigmoid(y)

    def l2n(t):
        return t * lax.rsqrt(jnp.sum(t * t, axis=-1, keepdims=True) + EPS)

    def prep(cb, carry):
        ks = range(_DN_CHUNKS_PER_ITER)
        units = [(k, h) for k in ks for h in range(H)]
        cid = [cb * _DN_CHUNKS_PER_ITER + k for k in ks]
        r0s = [pl.multiple_of(c * C, C) for c in cid]
        rps = [pl.multiple_of(jnp.maximum(r - 2 * SUBLANES, 0), 2 * SUBLANES) for r in r0s]
        bas = [ba_ref[0, pl.ds(r, C), :] for r in r0s]
        betas = [jax.nn.sigmoid(b) for b in bas]
        gs = [a_neg * _softplus(b + dtb) for b in bas]
        gcum = [jnp.dot(tril_f, g, precision=HI, preferred_element_type=F32)
                for g in gs]
        gcum_t = [lax.dot_general(g, triu_f, (((0,), (0,)), ((), ())), precision=HI,
                                  preferred_element_type=F32) for g in gs]
        q_c = [l2n(conv_silu(q_ref, cq_ref, h, r0s[c], rps[c], cid[c])) * (DN_HEAD_DIM ** -0.5)
               for c, h in units]
        k_c = [l2n(conv_silu(k_ref, ck_ref, h, r0s[c], rps[c], cid[c])) for c, h in units]
        v_c = [conv_silu(v_ref, cv_ref, h, r0s[c], rps[c], cid[c]) for c, h in units]
        bcol = [betas[c][:, h:h + 1] for c, h in units]
        gcol = [gcum[c][:, H + h:H + h + 1] for c, h in units]
        grow = [gcum_t[c][H + h:H + h + 1, :] for c, h in units]
        decay = [jnp.exp(jnp.where(tril, gc - gr, NEG_INF)) for gc, gr in zip(gcol, grow)]
        kb = [k * b for k, b in zip(k_c, bcol)]
        kk = [_dot3_nt(a, b) for a, b in zip(kb, k_c)]
        qk = [_dot_nt(a.astype(BF16), b.astype(BF16)) for a, b in zip(q_c, k_c)]
        m_k = [-jnp.where(strict, x * d, 0.0) for x, d in zip(kk, decay)]
        a_mat = [x * d for x, d in zip(qk, decay)]
        inv = [eye_f + m for m in m_k]
        mb = [m.astype(BF16) for m in m_k]
        m_k = [_dot(m, m) for m in mb]
        for lvl in range(1, 6):
            mb = [m.astype(BF16) for m in m_k]
            if lvl < 5:
                prod = [_dot(jnp.concatenate([iv, m], axis=0).astype(BF16), b)
                        for iv, m, b in zip(inv, m_k, mb)]
                m_k = [p[C:] for p in prod]
            else:
                prod = [_dot(iv.astype(BF16), b) for iv, b in zip(inv, mb)]
            inv = [iv + p[:C] for iv, p in zip(inv, prod)]
        eg = [jnp.exp(gc) for gc in gcol]
        rhs = [jnp.concatenate([v * b, x * e], axis=-1) for v, b, x, e in zip(v_c, bcol, kb, eg)]
        sol = [_dot(iv.astype(BF16), r.astype(BF16)) for iv, r in zip(inv, rhs)]
        for n, (c, h) in enumerate(units):
            glast = gcol[n][C - 1:C, :]
            rows = pl.ds(r0s[c], C)
            u_s[h, rows, :] = sol[n][:, :LANES]
            w_s[h, rows, :] = sol[n][:, LANES:].astype(BF16)
            qg_s[h, rows, :] = (q_c[n] * eg[n]).astype(BF16)
            kd_s[h, rows, :] = (k_c[n] * jnp.exp(glast - gcol[n])).astype(BF16)
            a_s[h, rows, :] = a_mat[n].astype(BF16)
            e_s[h, pl.ds(pl.multiple_of(cid[c] * SUBLANES, SUBLANES), SUBLANES), :] = (
                jnp.broadcast_to(jnp.exp(glast), (SUBLANES, LANES)))
        return carry

    lax.fori_loop(0, n_chunks // _DN_CHUNKS_PER_ITER, prep, 0)

    og = og_ref[...]
    st_s[...] = jnp.zeros_like(st_s)

    def scan(c, carry):
        r0 = pl.multiple_of(c * C, C)
        rows = pl.ds(r0, C)
        hs = range(H)
        state = [st_s[h] for h in hs]
        sb = [s.astype(BF16) for s in state]
        v_new = [u_s[h, rows, :] - _dot(w_s[h, rows, :], sb[h]) for h in hs]
        vb = [v.astype(BF16) for v in v_new]
        o = [_dot(qg_s[h, rows, :], sb[h]) + _dot(a_s[h, rows, :], vb[h]) for h in hs]
        upd = [_dot_tn(kd_s[h, rows, :], vb[h]) for h in hs]
        for h in hs:
            e_last = e_s[h, pl.ds(pl.multiple_of(c * SUBLANES, SUBLANES), 1), :]
            st_s[h] = state[h] * e_last + upd[h]
            on = o[h] * lax.rsqrt(jnp.mean(o[h] * o[h], axis=-1, keepdims=True) + EPS) * og
            z = z_ref[0, rows, h * LANES:(h + 1) * LANES].astype(F32)
            o_ref[0, rows, h * LANES:(h + 1) * LANES] = (
                on * (z * jax.nn.sigmoid(z))).astype(o_ref.dtype)
        return carry

    lax.fori_loop(0, n_chunks, scan, 0)


def _deltanet(qkvb, z, ba, conv_w, a_log, dt_bias, out_gain, B, S, bw):
    H = bw // DN_HEAD_DIM
    n_chunks = S // CHUNK
    qkv3 = qkvb.reshape(B, S, 3 * bw)
    z3 = z.reshape(B, S, bw)
    ba3 = ba.reshape(B, S, LANES)
    alog = jnp.zeros((1, LANES), F32).at[0, H:2 * H].set(a_log.astype(F32))
    dtb = jnp.zeros((1, LANES), F32).at[0, H:2 * H].set(dt_bias.astype(F32))
    blk = (1, S, bw)
    wblk = (CONV_WIDTH, bw)
    return pl.pallas_call(
        functools.partial(_dn_kernel, n_chunks=n_chunks, n_heads=H),
        grid=(B,),
        in_specs=[pl.BlockSpec(blk, lambda b: (b, 0, 0)),
                  pl.BlockSpec(blk, lambda b: (b, 0, 1)),
                  pl.BlockSpec(blk, lambda b: (b, 0, 2)),
                  pl.BlockSpec(blk, lambda b: (b, 0, 0)),
                  pl.BlockSpec((1, S, LANES), lambda b: (b, 0, 0)),
                  pl.BlockSpec(wblk, lambda b: (0, 0)),
                  pl.BlockSpec(wblk, lambda b: (0, 1)),
                  pl.BlockSpec(wblk, lambda b: (0, 2)),
                  pl.BlockSpec((1, LANES), lambda b: (0, 0)),
                  pl.BlockSpec((1, LANES), lambda b: (0, 0)),
                  pl.BlockSpec((1, LANES), lambda b: (0, 0))],
        out_specs=pl.BlockSpec(blk, lambda b: (b, 0, 0)),
        out_shape=jax.ShapeDtypeStruct((B, S, bw), BF16),
        scratch_shapes=[pltpu.VMEM((H, S, LANES), F32),
                        pltpu.VMEM((H, S, LANES), BF16),
                        pltpu.VMEM((H, S, LANES), BF16),
                        pltpu.VMEM((H, S, LANES), BF16),
                        pltpu.VMEM((H, S, CHUNK), BF16),
                        pltpu.VMEM((H, n_chunks * SUBLANES, LANES), F32),
                        pltpu.VMEM((H, DN_HEAD_DIM, DN_HEAD_DIM), F32)],
        compiler_params=_cparams("arbitrary"),
    )(qkv3, qkv3, qkv3, z3, ba3, conv_w, conv_w, conv_w, alog, dtb,
      out_gain.astype(F32).reshape(1, LANES))


def _outq_kernel(oa_ref, ob_ref, x_ref, mod_ref, ag_ref, wo_a_ref, wo_b_ref, n2_ref, wq_ref,
                 qgain_ref, k1_ref, k2_ref, x1_ref, h2_ref, st_ref, *, key_dim):
    oa = oa_ref[...]
    oan = oa * lax.rsqrt(jnp.mean(oa * oa, axis=-1, keepdims=True) + EPS) * ag_ref[...]
    mix = _dot(oan.astype(BF16), wo_a_ref[...]) + _dot(ob_ref[...].astype(BF16), wo_b_ref[...])
    gate1 = mod_ref[0, 2:3, :]
    shift2 = mod_ref[0, 3:4, :]
    scale2 = mod_ref[0, 4:5, :]
    x1 = x_ref[...] + gate1 * mix
    x1_ref[...] = x1
    y = x1 * lax.rsqrt(jnp.mean(x1 * x1, axis=-1, keepdims=True) + EPS) * n2_ref[...]
    h2 = (y * (1.0 + scale2) + shift2).astype(BF16)
    h2_ref[...] = h2
    qry = _dot(h2, wq_ref[...])
    half = key_dim // 2
    n_heads = qry.shape[1] // key_dim
    k1 = k1_ref[...]
    k2 = k2_ref[...]
    for h in range(n_heads):
        qh = qry[:, h * key_dim:(h + 1) * key_dim]
        qh = qh * lax.rsqrt(jnp.mean(qh * qh, axis=-1, keepdims=True) + EPS) * qgain_ref[...]
        st_ref[(2 * h) * N_KEYS:(2 * h + 1) * N_KEYS, :] = _dot3_nt(k1, qh[:, :half])
        st_ref[(2 * h + 1) * N_KEYS:(2 * h + 2) * N_KEYS, :] = _dot3_nt(k2, qh[:, half:])


def _outq(o_a, o_b, x2, mod3, attn_out_gain, w_out, norm2_gain, w_query, query_gain, k1, k2, S):
    T, D = x2.shape
    aw = o_a.shape[1]
    bw = o_b.shape[1]
    tm = min(_ROW_TILE, S)
    steps_per_b = S // tm
    key_dim = query_gain.shape[0]
    nq = w_query.shape[1]
    n_rows = (nq // key_dim) * 2 * N_KEYS
    wo_a = w_out[:aw].astype(BF16)
    wo_b = w_out[aw:].astype(BF16)
    wq = w_query.astype(BF16)
    const = lambda i: (0, 0)
    row = lambda i: (i, 0)
    return pl.pallas_call(
        functools.partial(_outq_kernel, key_dim=key_dim),
        grid=(T // tm,),
        in_specs=[pl.BlockSpec((tm, aw), row),
                  pl.BlockSpec((tm, bw), row),
                  pl.BlockSpec((tm, D), row),
                  pl.BlockSpec((1, 6, D), lambda i: (i // steps_per_b, 0, 0)),
                  pl.BlockSpec((1, aw), const),
                  pl.BlockSpec(wo_a.shape, const),
                  pl.BlockSpec(wo_b.shape, const),
                  pl.BlockSpec((1, D), const),
                  pl.BlockSpec(wq.shape, const),
                  pl.BlockSpec((1, key_dim), const),
                  pl.BlockSpec(k1.shape, const),
                  pl.BlockSpec(k2.shape, const)],
        out_specs=[pl.BlockSpec((tm, D), row),
                   pl.BlockSpec((tm, D), row),
                   pl.BlockSpec((n_rows, tm), lambda i: (0, i))],
        out_shape=[jax.ShapeDtypeStruct((T, D), F32),
                   jax.ShapeDtypeStruct((T, D), BF16),
                   jax.ShapeDtypeStruct((n_rows, T), F32)],
        compiler_params=_cparams("arbitrary"),
    )(o_a, o_b, x2, mod3, attn_out_gain.reshape(1, aw), wo_a, wo_b, norm2_gain.reshape(1, D), wq,
      query_gain.reshape(1, key_dim), k1, k2)


def _oddeven_merge_sort_pairs(n):
    pairs = []
    p = 1
    while p < n:
        k = p
        while k >= 1:
            for j in range(k % p, n - k, 2 * k):
                for i in range(min(k, n - j - k)):
                    if (i + j) // (2 * p) == (i + j + k) // (2 * p):
                        pairs.append((i + j, i + j + k))
            k //= 2
        p *= 2
    return pairs


_SORT16 = _oddeven_merge_sort_pairs(PEER_TOPK)
_BIG_INDEX = 1e9


def _sublane_all(x, op):
    for shift in (4, 2, 1):
        x = op(x, pltpu.roll(x, shift, 0))
    return x


def _pop_columns(vals, pays, tie_rank, extra=None):
    vals, pays = list(vals), list(pays)
    out_v, out_p = [], []
    for k in range(PEER_TOPK):
        head = vals[0] if extra is None else jnp.maximum(vals[0], extra[0])
        m = _sublane_all(head, jnp.maximum)
        cand = jnp.where(vals[0] == m, tie_rank, _BIG_INDEX)
        if extra is not None:
            cand = jnp.minimum(cand, jnp.where(extra[0] == m, extra[2], _BIG_INDEX))
        win = _sublane_all(cand, jnp.minimum)
        hit = tie_rank == win
        pay = jnp.where(hit, pays[0], 0.0)
        if extra is not None:
            hit_x = extra[2] == win
            pay = pay + jnp.where(hit_x, extra[1], 0.0)
            extra = (jnp.where(hit_x, NEG_INF, extra[0]), extra[1], extra[2])
        out_v.append(m[0:1])
        out_p.append(_sublane_all(pay, jnp.add)[0:1])
        for r in range(PEER_TOPK - 1 - k):
            vals[r] = jnp.where(hit, vals[r + 1], vals[r])
            pays[r] = jnp.where(hit, pays[r + 1], pays[r])
    return jnp.concatenate(out_v, axis=0), jnp.concatenate(out_p, axis=0)


def _top16(s):
    L = s.shape[1]
    sub = lax.broadcasted_iota(jnp.int32, (SUBLANES, L), 0).astype(F32)
    v = [s[r * SUBLANES:(r + 1) * SUBLANES] for r in range(PEER_TOPK)]
    ix = [sub + float(r * SUBLANES) for r in range(PEER_TOPK)]
    for i, j in _SORT16:
        swap = (v[j] > v[i]) | ((v[j] == v[i]) & (ix[j] < ix[i]))
        v[i], v[j] = jnp.where(swap, v[j], v[i]), jnp.where(swap, v[i], v[j])
        ix[i], ix[j] = jnp.where(swap, ix[j], ix[i]), jnp.where(swap, ix[i], ix[j])
    vals, idxs = [], []
    v, ix = list(v), list(ix)
    for k in range(PEER_TOPK):
        m = _sublane_all(v[0], jnp.maximum)
        win = _sublane_all(jnp.where(v[0] == m, ix[0], _BIG_INDEX), jnp.minimum)
        hit = ix[0] == win
        vals.append(m[0:1])
        idxs.append(win[0:1])
        for r in range(PEER_TOPK - 1 - k):
            v[r] = jnp.where(hit, v[r + 1], v[r])
            ix[r] = jnp.where(hit, ix[r + 1], ix[r])
    return jnp.concatenate(vals, axis=0), jnp.concatenate(idxs, axis=0)


def _route_kernel(st_ref, ii_ref, jj_ref, gate_ref, tr_s, *, n_groups):
    K = PEER_TOPK
    slots = PEER_HEADS * PEER_TOPK
    inv_keys = 1.0 / N_KEYS
    p_low = lax.broadcasted_iota(jnp.int32, (SUBLANES, LANES), 0).astype(F32)
    p_high = p_low + float(SUBLANES)

    def group(g, carry):
        l0 = pl.multiple_of(g * LANES, LANES)
        i_rows, j_rows, g_rows = [], [], []
        for h in range(PEER_HEADS):
            s1 = st_ref[(2 * h) * N_KEYS:(2 * h + 1) * N_KEYS, pl.ds(l0, LANES)]
            s2 = st_ref[(2 * h + 1) * N_KEYS:(2 * h + 2) * N_KEYS, pl.ds(l0, LANES)]
            a, ia = _top16(s1)
            b, ib = _top16(s2)
            a8, ia8 = a[:SUBLANES], ia[:SUBLANES] * N_KEYS
            cv = [a8 + b[q:q + 1] for q in range(K)]
            ce = [ia8 + ib[q:q + 1] for q in range(K)]
            extra = (a[SUBLANES:] + b[0:1], ia[SUBLANES:] * N_KEYS + ib[0:1], p_high)
            ts, te = _pop_columns(cv, ce, p_low, extra)
            ex = jnp.exp(ts - ts[0:1])
            g_rows.append(ex / jnp.sum(ex, axis=0, keepdims=True))
            ti = jnp.floor(te * inv_keys)
            i_rows.append(ti)
            j_rows.append(te - ti * N_KEYS)
        p0 = pl.multiple_of(g * (LANES // 2), LANES // 2)
        for rows, out_ref in ((i_rows, ii_ref), (j_rows, jj_ref), (g_rows, gate_ref)):
            tr_s[...] = jnp.concatenate(rows, axis=0).T
            out_ref[pl.ds(p0, LANES // 2), 0:slots] = tr_s[pl.ds(0, LANES // 2, stride=2), :]
            out_ref[pl.ds(p0, LANES // 2), slots:2 * slots] = tr_s[pl.ds(1, LANES // 2, stride=2), :]
        return carry

    lax.fori_loop(0, n_groups, group, 0)


def _route(st):
    n_rows, T = st.shape
    tm = min(_ROW_TILE, T)
    slots = PEER_HEADS * PEER_TOPK
    out = jax.ShapeDtypeStruct((T // 2, 2 * slots), F32)
    row = lambda i: (i, 0)
    return pl.pallas_call(
        functools.partial(_route_kernel, n_groups=tm // LANES),
        grid=(T // tm,),
        in_specs=[pl.BlockSpec((n_rows, tm), lambda i: (0, i))],
        out_specs=[pl.BlockSpec((tm // 2, 2 * slots), row)] * 3,
        out_shape=[out, out, out],
        scratch_shapes=[pltpu.VMEM((LANES, slots), F32)],
        compiler_params=_cparams("arbitrary"),
    )(st)


_MXU_COLS = 256
_PEER_BUILD_UNROLL = 64
_G_PITCH = N_KEYS + SUBLANES


def _peer_kernel(h2_ref, ii_ref, jj_ref, gate_ref, dnt_ref, up_ref, x1_ref, mod_ref, o_ref,
                 g_s, m_s, *, tm, te):
    j = pl.program_id(1)
    n_i = te // N_KEYS
    n_col = te // _MXU_COLS
    slots2 = ii_ref.shape[1]
    one = jnp.ones((), BF16)
    zero = jnp.zeros((), BF16)

    @pl.when(j == 0)
    def _():
        o_ref[...] = jnp.zeros_like(o_ref)
        pair_row = lax.broadcasted_iota(jnp.int32, (2 * N_KEYS, slots2), 0).astype(F32).astype(BF16)
        key_row = lax.broadcasted_iota(jnp.int32, (N_KEYS, slots2), 0).astype(F32).astype(BF16)
        odd = (lax.broadcasted_iota(jnp.int32, (1, slots2), 1) >= slots2 // 2).astype(F32)

        def build(tp, carry):
            code = (2.0 * ii_ref[pl.ds(tp, 1), :] + odd).astype(BF16)
            jj = jj_ref[pl.ds(tp, 1), :].astype(BF16)
            gt = gate_ref[pl.ds(tp, 1), :].astype(BF16)
            a = jnp.where(pair_row == code, one, zero)
            c = jnp.where(key_row == jj, gt, zero)
            g2 = _dot_nt(a, c).astype(BF16)
            g_s[pl.ds(pl.multiple_of(tp * _G_PITCH, SUBLANES), N_KEYS), :] = pltpu.bitcast(g2, jnp.uint32)
            return carry

        lax.fori_loop(0, tm // 2, build, 0, unroll=_PEER_BUILD_UNROLL)

    h2 = h2_ref[...]
    for cc in range(n_col):
        act = _dot(h2, dnt_ref[:, cc * _MXU_COLS:(cc + 1) * _MXU_COLS])
        for il in range(_MXU_COLS // N_KEYS):
            a = act[:, il * N_KEYS:(il + 1) * N_KEYS]
            i_loc = cc * (_MXU_COLS // N_KEYS) + il
            gw = g_s[pl.ds(j * n_i + i_loc, tm // 2, stride=_G_PITCH), :]
            g = pltpu.bitcast(gw, BF16).astype(F32)
            gelu = 0.5 * a * (1.0 + lax.erf(a * (2.0 ** -0.5)))
            m_s[:, i_loc * N_KEYS:(i_loc + 1) * N_KEYS] = (g * gelu).astype(BF16)
    half = te // 2
    dh = o_ref.shape[1] // 2
    parts = [_dot(m_s[:, ks:ks + half], up_ref[ks:ks + half, ns:ns + dh])
             for ks in (0, half) for ns in (0, dh)]
    o_ref[:, :dh] += parts[0] + parts[2]
    o_ref[:, dh:] += parts[1] + parts[3]

    @pl.when(j == pl.num_programs(1) - 1)
    def _():
        o_ref[...] = x1_ref[...] + mod_ref[0, 5:6, :] * o_ref[...]


def _peer(h2, ii, jj, gate, expert_down, expert_up, x1, mod3, S):
    T, D = h2.shape
    E = expert_down.shape[0]
    tm = min(_PEER_TOKEN_TILE, S)
    te = _PEER_EXPERT_TILE
    steps_per_b = S // tm
    slots2 = ii.shape[1]
    dnt = expert_down.astype(BF16).T
    up = expert_up.astype(BF16)
    return pl.pallas_call(
        functools.partial(_peer_kernel, tm=tm, te=te),
        grid=(T // tm, E // te),
        in_specs=[pl.BlockSpec((tm, D), lambda i, j: (i, 0)),
                  pl.BlockSpec((tm // 2, slots2), lambda i, j: (i, 0)),
                  pl.BlockSpec((tm // 2, slots2), lambda i, j: (i, 0)),
                  pl.BlockSpec((tm // 2, slots2), lambda i, j: (i, 0)),
                  pl.BlockSpec((D, te), lambda i, j: (0, j)),
                  pl.BlockSpec((te, D), lambda i, j: (j, 0)),
                  pl.BlockSpec((tm, D), lambda i, j: (i, 0)),
                  pl.BlockSpec((1, 6, D), lambda i, j: (i // steps_per_b, 0, 0))],
        out_specs=pl.BlockSpec((tm, D), lambda i, j: (i, 0)),
        out_shape=jax.ShapeDtypeStruct((T, D), F32),
        scratch_shapes=[pltpu.VMEM((tm // 2 * _G_PITCH, N_KEYS), jnp.uint32),
                        pltpu.VMEM((tm, te), BF16)],
        compiler_params=_cparams("arbitrary", "arbitrary"),
    )(h2, ii, jj, gate, dnt, up, x1, mod3)


def _layer(x, c, w_ada, b_ada, norm1_gain, w_in, attn_q_gain, attn_k_gain, attn_rel_bias,
           attn_out_gain, dn_conv_w, dn_a_log, dn_dt_bias, dn_out_gain, w_out, norm2_gain,
           peer_w_query, peer_query_gain, peer_sub_keys_1, peer_sub_keys_2,
           peer_expert_down, peer_expert_up):
    B, S, D = x.shape
    aw = attn_out_gain.shape[0]
    bw = w_out.shape[0] - aw
    n_small = w_in.shape[1] - 3 * aw - 4 * bw
    x2 = x.reshape(B * S, D)
    mod3 = _ada(c, w_ada, b_ada).reshape(B, 6, D)
    qkva, qkvb, z, ba = _inproj(x2, mod3, norm1_gain, w_in, S, aw, bw, n_small)
    o_a = _attention(qkva, attn_q_gain, attn_k_gain, attn_rel_bias, B, S, aw)
    o_b = _deltanet(qkvb, z, ba, dn_conv_w, dn_a_log, dn_dt_bias, dn_out_gain, B, S, bw)
    x1, h2, st = _outq(o_a.reshape(B * S, aw), o_b.reshape(B * S, bw), x2, mod3, attn_out_gain,
                       w_out, norm2_gain, peer_w_query, peer_query_gain, peer_sub_keys_1,
                       peer_sub_keys_2, S)
    ii, jj, gate = _route(st)
    out = _peer(h2, ii, jj, gate, peer_expert_down, peer_expert_up, x1, mod3, S)
    return out.reshape(B, S, D)


def kernel(x, c, w_ada, b_ada, norm1_gain, w_in, attn_q_gain, attn_k_gain, attn_rel_bias,
           attn_out_gain, dn_conv_w, dn_a_log, dn_dt_bias, dn_out_gain, w_out, norm2_gain,
           peer_w_query, peer_query_gain, peer_sub_keys_1, peer_sub_keys_2, peer_expert_down,
           peer_expert_up):
    for layer in range(w_ada.shape[0]):
        x = _layer(x, c, w_ada[layer], b_ada[layer], norm1_gain[layer], w_in[layer],
                   attn_q_gain[layer], attn_k_gain[layer], attn_rel_bias[layer],
                   attn_out_gain[layer], dn_conv_w[layer], dn_a_log[layer], dn_dt_bias[layer],
                   dn_out_gain[layer], w_out[layer], norm2_gain[layer], peer_w_query[layer],
                   peer_query_gain[layer], peer_sub_keys_1[layer], peer_sub_keys_2[layer],
                   peer_expert_down[layer], peer_expert_up[layer])
    return x
```
